```python
import jax, jax.numpy as jnp
from jax import lax
import numpy as np

D_MODEL = 1024
BATCH = 32
SEQ = 2048
DEPTH = 2

MEM_LEN = 256
D_MIX = 2 * D_MODEL
HG_DK = 128
HG_DV = 128
HG_WIDTH = 3 * D_MODEL // 4
HG_HEADS = HG_WIDTH // HG_DK
HG_CHUNK = 32
ML_WIDTH = 3 * D_MODEL // 4
ML_HEADS = 4
ML_DH = ML_WIDTH // ML_HEADS
ML_BLOCK = 4
ML_NBLK = ML_WIDTH // ML_BLOCK
ML_CONV = 4
ML_CHUNK = 64
XA_WIDTH = D_MODEL // 2
XA_HEADS = 4
XA_DH = XA_WIDTH // XA_HEADS

P_IN = 4 * HG_WIDTH + 2 * ML_WIDTH + 2 * XA_WIDTH
SPLITS = tuple(np.cumsum([HG_WIDTH] * 4 + [ML_WIDTH] * 2 + [XA_WIDTH] * 2)[:-1].tolist())
EPS = 1e-6
M_INIT = -1e30

kernel_name = "hymba_hgrn2_mlstm_memxattn"


def rms_norm(x, w):
    xf = x.astype(jnp.float32)
    y = xf * lax.rsqrt(jnp.mean(xf * xf, axis=-1, keepdims=True) + EPS)
    return (y * w.astype(jnp.float32)).astype(x.dtype)


def split_heads(t, h):
    b, s, _ = t.shape
    return t.reshape(b, s, h, -1).transpose(0, 2, 1, 3)


def merge_heads(t):
    b, h, s, d = t.shape
    return t.transpose(0, 2, 1, 3).reshape(b, s, h * d)


def to_chunks(t, c):
    b, h, s = t.shape[:3]
    return jnp.moveaxis(t.reshape(b, h, s // c, c, *t.shape[3:]), 2, 0)


def from_chunks(t):
    t = jnp.moveaxis(t, 0, 2)
    return t.reshape(t.shape[0], t.shape[1], -1, t.shape[-1])


def head_rms(t, h, w):
    b, s, _ = t.shape
    th = t.astype(jnp.float32).reshape(b, s, h, -1)
    th = th * lax.rsqrt(jnp.mean(th * th, axis=-1, keepdims=True) + EPS) * w.astype(jnp.float32)
    return th.reshape(b, s, -1)


def causal_conv(x, w, bias):
    k, c = w.shape
    y = lax.conv_general_dilated(x, w[:, None, :], window_strides=(1,), padding=[(k - 1, 0)],
                                 dimension_numbers=('NWC', 'WIO', 'NWC'), feature_group_count=c)
    return y + bias


def headwise(x, w):
    b, s, _ = x.shape
    y = jnp.einsum('btgi,gio->btgo', x.reshape(b, s, ML_NBLK, ML_BLOCK), w)
    return y.reshape(b, s, ML_WIDTH)


def hgrn2_mixer(q, f_pre, i, lb):
    f32 = jnp.float32
    bsz = q.shape[0]
    q = jax.nn.silu(q.astype(f32)) * HG_DK ** -0.5
    f_pre = f_pre.astype(f32)
    lb = lb.astype(f32)
    log_f = jnp.logaddexp(jnp.log(lb), jnp.log1p(-lb) + jax.nn.log_sigmoid(f_pre))
    k = (1.0 - lb) * jax.nn.sigmoid(-f_pre)
    v = i.astype(f32)
    qc, kc, gc = (to_chunks(split_heads(t, HG_HEADS), HG_CHUNK) for t in (q, k, log_f))
    vc = to_chunks(split_heads(v, HG_HEADS), HG_CHUNK)
    causal = jnp.tril(jnp.ones((HG_CHUNK, HG_CHUNK), dtype=bool))

    def step(S, xs):
        qb, kb, vb, gb = xs
        bcum = jnp.cumsum(gb, axis=2)
        dec = bcum[:, :, :, None, :] - bcum[:, :, None, :, :]
        dec = jnp.exp(jnp.where(causal[:, :, None], dec, -jnp.inf))
        A = jnp.einsum('bhtd,bhsd,bhtsd->bhts', qb, kb, dec)
        o = (jnp.einsum('bhts,bhsv->bhtv', A, vb)
             + jnp.einsum('bhtd,bhdv->bhtv', qb * jnp.exp(bcum), S))
        blast = bcum[:, :, -1:, :]
        S = (jnp.exp(blast[:, :, 0, :, None]) * S
             + jnp.einsum('bhsd,bhsv->bhdv', kb * jnp.exp(blast - bcum), vb))
        return S, o

    S0 = jnp.zeros((bsz, HG_HEADS, HG_DK, HG_DV), f32)
    _, o = lax.scan(step, S0, (qc, kc, vc, gc))
    return merge_heads(from_chunks(o))


def mlstm_mixer(xm, conv_w, conv_b, wq, wk, wv, w_gate, b_gate, norm_w, skip):
    f32 = jnp.float32
    bsz = xm.shape[0]
    xc = jax.nn.silu(causal_conv(xm, conv_w, conv_b))
    q = headwise(xc, wq)
    k = headwise(xc, wk)
    v = headwise(xm, wv)
    gates = (jnp.concatenate([q, k, v], axis=-1) @ w_gate + b_gate).astype(f32)
    i_pre = gates[..., :ML_HEADS].transpose(0, 2, 1)
    log_f = jax.nn.log_sigmoid(gates[..., ML_HEADS:]).transpose(0, 2, 1)
    qh = split_heads(q.astype(f32), ML_HEADS)
    kh = split_heads(k.astype(f32), ML_HEADS) * ML_DH ** -0.5
    vh = split_heads(v.astype(f32), ML_HEADS)
    qc, kc, vc = (to_chunks(t, ML_CHUNK) for t in (qh, kh, vh))
    ic, fc = (to_chunks(t, ML_CHUNK) for t in (i_pre, log_f))
    causal = jnp.tril(jnp.ones((ML_CHUNK, ML_CHUNK), dtype=bool))

    def step(carry, xs):
        Cm, n, m = carry
        qb, kb, vb, ib, fb = xs
        bcum = jnp.cumsum(fb, axis=-1)
        D = jnp.where(causal, bcum[..., :, None] - bcum[..., None, :] + ib[..., None, :], -jnp.inf)
        inter = bcum + m[..., None]
        m_t = jnp.maximum(inter, jnp.max(D, axis=-1))
        w_intra = jnp.exp(D - m_t[..., None])
        w_inter = jnp.exp(inter - m_t)
        Sm = jnp.einsum('bhtd,bhsd->bhts', qb, kb) * w_intra
        num = (jnp.einsum('bhts,bhsv->bhtv', Sm, vb)
               + w_inter[..., None] * jnp.einsum('bhtd,bhdv->bhtv', qb, Cm))
        den = jnp.sum(Sm, axis=-1) + w_inter * jnp.einsum('bhtd,bhd->bht', qb, n)
        h = num / jnp.maximum(jnp.abs(den), jnp.exp(-m_t))[..., None]
        blast = bcum[..., -1]
        g_s = blast[..., None] - bcum + ib
        m_new = jnp.maximum(blast + m, jnp.max(g_s, axis=-1))
        w_old = jnp.exp(blast + m - m_new)
        kw = kb * jnp.exp(g_s - m_new[..., None])[..., None]
        Cm = w_old[..., None, None] * Cm + jnp.einsum('bhsd,bhsv->bhdv', kw, vb)
        n = w_old[..., None] * n + jnp.sum(kw, axis=2)
        return (Cm, n, m_new), h

    init = (jnp.zeros((bsz, ML_HEADS, ML_DH, ML_DH), f32),
            jnp.zeros((bsz, ML_HEADS, ML_DH), f32),
            jnp.full((bsz, ML_HEADS), M_INIT, f32))
    _, h = lax.scan(step, init, (qc, kc, vc, ic, fc))
    h = from_chunks(h)
    mu = jnp.mean(h, axis=-1, keepdims=True)
    var = jnp.mean(jnp.square(h - mu), axis=-1, keepdims=True)
    h = merge_heads((h - mu) * lax.rsqrt(var + EPS))
    return h * norm_w.astype(f32) + skip.astype(f32) * xc.astype(f32)


def memory_xattn(q, mem_k, mem_v):
    bsz, s, _ = q.shape
    qh = q.reshape(bsz, s, XA_HEADS, XA_DH)
    kh = mem_k.reshape(bsz, MEM_LEN, XA_HEADS, XA_DH)
    vh = mem_v.reshape(bsz, MEM_LEN, XA_HEADS, XA_DH)
    scores = jnp.einsum('bthd,bmhd->bhtm', qh, kh).astype(jnp.float32) * XA_DH ** -0.5
    p = jax.nn.softmax(scores, axis=-1)
    o = jnp.einsum('bhtm,bmhd->bthd', p, vh.astype(jnp.float32))
    return o.reshape(bsz, s, XA_WIDTH)


def setup_inputs(seed: int = 0) -> dict:
    key = jax.random.key(seed)
    ks = jax.random.split(key, 24)
    nrm = lambda k, shape, s: jax.random.normal(k, shape, jnp.float32) * s
    f_bias = jnp.linspace(3.0, 6.0, ML_HEADS, dtype=jnp.float32)[None, :] + nrm(ks[13], (DEPTH, ML_HEADS), 0.05)
    return {
        "x": nrm(ks[0], (BATCH, SEQ, D_MODEL), 1.0),
        "mem": nrm(ks[1], (BATCH, MEM_LEN, D_MODEL), 1.0),
        "norm_w": 1.0 + nrm(ks[2], (DEPTH, D_MODEL), 0.02),
        "w_in": nrm(ks[3], (DEPTH, D_MODEL, P_IN), D_MODEL ** -0.5),
        "hgrn_lb_logits": nrm(ks[4], (DEPTH, HG_WIDTH), 0.1),
        "hgrn_norm_w": 1.0 + nrm(ks[5], (DEPTH, HG_DV), 0.02),
        "mlstm_conv_w": nrm(ks[6], (DEPTH, ML_CONV, ML_WIDTH), ML_CONV ** -0.5),
        "mlstm_conv_b": nrm(ks[7], (DEPTH, ML_WIDTH), 0.02),
        "mlstm_wq": nrm(ks[8], (DEPTH, ML_NBLK, ML_BLOCK, ML_BLOCK), ML_BLOCK ** -0.5),
        "mlstm_wk": nrm(ks[9], (DEPTH, ML_NBLK, ML_BLOCK, ML_BLOCK), ML_BLOCK ** -0.5),
        "mlstm_wv": nrm(ks[10], (DEPTH, ML_NBLK, ML_BLOCK, ML_BLOCK), ML_BLOCK ** -0.5),
        "mlstm_w_gate": nrm(ks[11], (DEPTH, 3 * ML_WIDTH, 2 * ML_HEADS), 0.3 * (3 * ML_WIDTH) ** -0.5),
        "mlstm_b_gate": jnp.concatenate([nrm(ks[12], (DEPTH, ML_HEADS), 0.1), f_bias], axis=-1),
        "mlstm_norm_w": 1.0 + nrm(ks[14], (DEPTH, ML_WIDTH), 0.02),
        "mlstm_skip": 1.0 + nrm(ks[15], (DEPTH, ML_WIDTH), 0.02),
        "mem_norm_w": 1.0 + nrm(ks[16], (DEPTH, D_MODEL), 0.02),
        "w_kv_mem": nrm(ks[17], (DEPTH, D_MODEL, 2 * XA_WIDTH), D_MODEL ** -0.5),
        "xattn_norm_w": 1.0 + nrm(ks[18], (DEPTH, XA_DH), 0.02),
        "w_out": nrm(ks[19], (DEPTH, D_MIX, D_MODEL), D_MIX ** -0.5),
        "final_norm_w": 1.0 + nrm(ks[20], (D_MODEL,), 0.02),
    }


def reference(x, mem, norm_w, w_in, hgrn_lb_logits, hgrn_norm_w, mlstm_conv_w, mlstm_conv_b,
              mlstm_wq, mlstm_wk, mlstm_wv, mlstm_w_gate, mlstm_b_gate, mlstm_norm_w, mlstm_skip,
              mem_norm_w, w_kv_mem, xattn_norm_w, w_out, final_norm_w):
    dt = x.dtype
    lb_all = jnp.cumsum(jax.nn.softmax(hgrn_lb_logits.astype(jnp.float32), axis=0), axis=0)
    lb_all = jnp.maximum(lb_all - lb_all[0:1], 0.0)
    for l in range(DEPTH):
        h = rms_norm(x, norm_w[l])
        proj = h @ w_in[l]
        hq, hf, hi, hg, mx, mz, aq, az = jnp.split(proj, SPLITS, axis=-1)
        o_hg = head_rms(hgrn2_mixer(hq, hf, hi, lb_all[l]), HG_HEADS, hgrn_norm_w[l]) * jax.nn.silu(hg)
        o_ml = mlstm_mixer(mx, mlstm_conv_w[l], mlstm_conv_b[l], mlstm_wq[l], mlstm_wk[l], mlstm_wv[l],
                           mlstm_w_gate[l], mlstm_b_gate[l], mlstm_norm_w[l], mlstm_skip[l]) * jax.nn.silu(mz)
        mkv = rms_norm(mem, mem_norm_w[l]) @ w_kv_mem[l]
        mk, mv = jnp.split(mkv, 2, axis=-1)
        o_xa = head_rms(memory_xattn(aq, mk, mv), XA_HEADS, xattn_norm_w[l]) * jax.nn.silu(az)
        y = jnp.concatenate([o_hg, o_ml, o_xa], axis=-1).astype(dt) @ w_out[l]
        x = x + y
    return rms_norm(x, final_norm_w)
```

```python
import functools

import jax
import jax.numpy as jnp
import numpy as np
from jax import lax
from jax.experimental import pallas as pl
from jax.experimental.pallas import tpu as pltpu

F32 = jnp.float32
BF16 = jnp.bfloat16

D_MODEL = 1024
DEPTH = 2
MEM_LEN = 256
HG_DK = 128
HG_WIDTH = 768
HG_HEADS = 6
ML_WIDTH = 768
ML_HEADS = 4
ML_DH = 192
ML_DHP = 256
ML_WP = ML_HEADS * ML_DHP
ML_BLOCK = 4
ML_NBLK = 192
ML_CONV = 4
XA_WIDTH = 512
XA_HEADS = 4
XA_DH = 128
EPS = 1e-6
M_INIT = -1e30

P_PAD = 4 * HG_WIDTH + 2 * ML_WP + 2 * XA_WIDTH
HG_COLBLK = HG_WIDTH // HG_DK
MX_BLK = (4 * HG_WIDTH) // ML_WP
MZ_BLK = MX_BLK + 1
AQ_BLK = (4 * HG_WIDTH + 2 * ML_WP) // XA_WIDTH
AZ_BLK = AQ_BLK + 1

SUBLANES = 8
CHUNK = 128
LEVELS = (8, 16, 32, 64)

ROW_TILE = 256
HG_TILE = 512
ML_TILE = 256
XA_TILE = 512
VMEM_LIMIT = 56 * 1024 * 1024

NT = (((1,), (1,)), ((), ()))
TN = (((0,), (0,)), ((), ()))


def _silu(x):
    return x / (1.0 + jnp.exp(-x))


def _log_sigmoid(x):
    return jnp.minimum(x, 0.0) - jnp.log1p(jnp.exp(-jnp.abs(x)))


def _bcast_block_row(x, blk, r):
    n = x.shape[0] // blk
    x3 = x.reshape(n, blk, x.shape[1])
    return jnp.broadcast_to(x3[:, r:r + 1, :], x3.shape).reshape(x.shape)


def _prefix8(g, row):
    p = g
    for sh in (1, 2, 4):
        rolled = pltpu.roll(p, sh, axis=0)
        p = p + jnp.where((row % SUBLANES) >= sh, rolled, 0.0)
    return p


def _prefix_levels(g, row):
    p = _prefix8(g, row)
    out = []
    for m in LEVELS:
        lt = _bcast_block_row(p, 2 * m, m - 1)
        out.append((m, p, lt))
        p = p + jnp.where((row % (2 * m)) >= m, lt, 0.0)
    return out, p


def _inproj_kernel(x_ref, nw_ref, w_ref, o_ref):
    x = x_ref[...]
    ms = jnp.mean(x * x, axis=-1, keepdims=True)
    h = x * lax.rsqrt(ms + EPS) * nw_ref[...]
    o_ref[...] = jnp.dot(h.astype(BF16), w_ref[...], preferred_element_type=F32)


def _inproj(x2, nw, w):
    n, d = x2.shape
    p = w.shape[1]
    return pl.pallas_call(
        _inproj_kernel,
        grid=(n // ROW_TILE,),
        in_specs=[
            pl.BlockSpec((ROW_TILE, d), lambda i: (i, 0)),
            pl.BlockSpec((1, d), lambda i: (0, 0)),
            pl.BlockSpec((d, p), lambda i: (0, 0)),
        ],
        out_specs=pl.BlockSpec((ROW_TILE, p), lambda i: (i, 0)),
        out_shape=jax.ShapeDtypeStruct((n, p), F32),
        compiler_params=pltpu.CompilerParams(
            dimension_semantics=("parallel",), vmem_limit_bytes=VMEM_LIMIT),
        name="inproj",
    )(x2, nw, w)


def _memkv_kernel(x_ref, nw_ref, w_ref, o_ref):
    x = x_ref[...]
    ms = jnp.mean(x * x, axis=-1, keepdims=True)
    h = x * lax.rsqrt(ms + EPS) * nw_ref[...]
    o_ref[...] = jnp.dot(h.astype(BF16), w_ref[...], preferred_element_type=F32).astype(o_ref.dtype)


def _memkv(mem2, nw, w):
    n, d = mem2.shape
    p = w.shape[1]
    return pl.pallas_call(
        _memkv_kernel,
        grid=(n // ROW_TILE,),
        in_specs=[
            pl.BlockSpec((ROW_TILE, d), lambda i: (i, 0)),
            pl.BlockSpec((1, d), lambda i: (0, 0)),
            pl.BlockSpec((d, p), lambda i: (0, 0)),
        ],
        out_specs=pl.BlockSpec((ROW_TILE, p), lambda i: (i, 0)),
        out_shape=jax.ShapeDtypeStruct((n, p), BF16),
        compiler_params=pltpu.CompilerParams(
            dimension_semantics=("parallel",), vmem_limit_bytes=VMEM_LIMIT),
        name="memkv",
    )(mem2, nw, w)


def _outproj_kernel(hg_ref, ml_ref, xa_ref, x_ref, w1_ref, w2_ref, w3_ref, fw_ref, o_ref, *, final):
    y = jnp.dot(hg_ref[...], w1_ref[...], preferred_element_type=F32)
    y = y + jnp.dot(ml_ref[...], w2_ref[...], preferred_element_type=F32)
    y = y + jnp.dot(xa_ref[...], w3_ref[...], preferred_element_type=F32)
    x = x_ref[...] + y
    if final:
        ms = jnp.mean(x * x, axis=-1, keepdims=True)
        x = x * lax.rsqrt(ms + EPS) * fw_ref[...]
    o_ref[...] = x


def _outproj(o_hg, o_ml, o_xa, x2, w1, w2, w3, fw, final):
    n, d = x2.shape
    row = lambda i: (i, 0)
    const = lambda i: (0, 0)
    return pl.pallas_call(
        functools.partial(_outproj_kernel, final=final),
        grid=(n // ROW_TILE,),
        in_specs=[
            pl.BlockSpec((ROW_TILE, o_hg.shape[1]), row),
            pl.BlockSpec((ROW_TILE, o_ml.shape[1]), row),
            pl.BlockSpec((ROW_TILE, o_xa.shape[1]), row),
            pl.BlockSpec((ROW_TILE, d), row),
            pl.BlockSpec(w1.shape, const),
            pl.BlockSpec(w2.shape, const),
            pl.BlockSpec(w3.shape, const),
            pl.BlockSpec((1, d), const),
        ],
        out_specs=pl.BlockSpec((ROW_TILE, d), row),
        out_shape=jax.ShapeDtypeStruct((n, d), F32),
        compiler_params=pltpu.CompilerParams(
            dimension_semantics=("parallel",), vmem_limit_bytes=VMEM_LIMIT),
        name="outproj",
    )(o_hg, o_ml, o_xa, x2, w1, w2, w3, fw)


def _hgrn_kernel(q_ref, f_ref, i_ref, g_ref, lb_ref, nw_ref, o_ref, st_ref):
    @pl.when(pl.program_id(2) == 0)
    def _():
        st_ref[...] = jnp.zeros_like(st_ref)

    lb = lb_ref[...]
    log_lb = jnp.log(lb)
    log_1m = jnp.log1p(-lb)
    row = lax.broadcasted_iota(jnp.int32, (CHUNK, HG_DK), 0)
    col = lax.broadcasted_iota(jnp.int32, (CHUNK, CHUNK), 1)
    rowc = lax.broadcasted_iota(jnp.int32, (CHUNK, CHUNK), 0)

    for c in range(q_ref.shape[0] // CHUNK):
        rows = pl.ds(c * CHUNK, CHUNK)
        q = _silu(q_ref[rows, :]) * (HG_DK ** -0.5)
        f_pre = f_ref[rows, :]
        v = i_ref[rows, :]
        e = jnp.exp(-jnp.abs(f_pre))
        k = (1.0 - lb) * (jnp.where(f_pre >= 0.0, e, 1.0) / (1.0 + e))
        c1 = log_1m + (jnp.minimum(f_pre, 0.0) - jnp.log1p(e))
        g = jnp.maximum(log_lb, c1) + jnp.log1p(jnp.exp(-jnp.abs(log_lb - c1)))

        levels, b = _prefix_levels(g, row)
        p8 = levels[0][1]
        v_bf = v.astype(BF16)

        a = jnp.zeros((CHUNK, CHUNK), F32)
        for m, pm, lt in levels:
            right = (row % (2 * m)) >= m
            ex = jnp.exp(jnp.where(right, pm, lt - pm))
            qe = jnp.where(right, q * ex, 0.0).astype(BF16)
            ke = jnp.where(right, 0.0, k * ex).astype(BF16)
            al = lax.dot_general(qe, ke, NT, preferred_element_type=F32)
            if 2 * m < CHUNK:
                al = jnp.where((rowc // (2 * m)) == (col // (2 * m)), al, 0.0)
            a = a + al
        o = jnp.dot(a.astype(BF16), v_bf, preferred_element_type=F32)

        for j in range(SUBLANES):
            bs = _bcast_block_row(p8, SUBLANES, j)
            ks = _bcast_block_row(k, SUBLANES, j)
            vs = _bcast_block_row(v, SUBLANES, j)
            ed = jnp.exp(jnp.where((row % SUBLANES) >= j, p8 - bs, -jnp.inf))
            aj = jnp.sum(q * ed * ks, axis=-1, keepdims=True)
            o = o + aj * vs

        st = st_ref[...]
        qb = (q * jnp.exp(b)).astype(BF16)
        o = o + lax.dot_general(qb, st.astype(BF16), NT, preferred_element_type=F32)
        blast = b[CHUNK - 1:CHUNK, :]
        kd = (k * jnp.exp(blast - b)).astype(BF16)
        st_ref[...] = st * jnp.exp(blast) + lax.dot_general(v_bf, kd, TN, preferred_element_type=F32)

        ms = jnp.mean(o * o, axis=-1, keepdims=True)
        o = o * lax.rsqrt(ms + EPS) * nw_ref[...]
        o_ref[rows, :] = (o * _silu(g_ref[rows, :])).astype(o_ref.dtype)


def _hgrn(proj, lb, nw):
    b, t, _ = proj.shape
    sec = lambda s: pl.BlockSpec((None, HG_TILE, HG_DK), lambda bi, h, ti, s=s: (bi, ti, s * HG_COLBLK + h))
    return pl.pallas_call(
        _hgrn_kernel,
        grid=(b, HG_HEADS, t // HG_TILE),
        in_specs=[
            sec(0), sec(1), sec(2), sec(3),
            pl.BlockSpec((1, HG_DK), lambda bi, h, ti: (0, h)),
            pl.BlockSpec((1, HG_DK), lambda bi, h, ti: (0, 0)),
        ],
        out_specs=pl.BlockSpec((None, HG_TILE, HG_DK), lambda bi, h, ti: (bi, ti, h)),
        out_shape=jax.ShapeDtypeStruct((b, t, HG_WIDTH), BF16),
        scratch_shapes=[pltpu.VMEM((HG_DK, HG_DK), F32)],
        compiler_params=pltpu.CompilerParams(
            dimension_semantics=("parallel", "parallel", "arbitrary"), vmem_limit_bytes=VMEM_LIMIT),
        name="hgrn2",
    )(proj, proj, proj, proj, lb, nw)


def _mlstm_kernel(mx_ref, mz_ref, cw_ref, cb_ref, wqk_ref, wv_ref, wg_ref, bg_ref, nw_ref, sk_ref,
                  o_ref, xbuf, cm_ref, n_ref, m_ref):
    tile = mx_ref.shape[0]

    @pl.when(pl.program_id(1) == 0)
    def _():
        xbuf[0:SUBLANES, :] = jnp.zeros((SUBLANES, ML_WP), F32)
        cm_ref[...] = jnp.zeros_like(cm_ref)
        n_ref[...] = jnp.zeros_like(n_ref)
        m_ref[...] = jnp.full(m_ref.shape, M_INIT, F32)

    xm = mx_ref[...]
    xbuf[SUBLANES:SUBLANES + tile, :] = xm
    conv = cb_ref[...]
    for j in range(ML_CONV):
        conv = conv + cw_ref[j:j + 1, :] * xbuf[pl.ds(SUBLANES - (ML_CONV - 1) + j, tile), :]
    xbuf[0:SUBLANES, :] = xm[tile - SUBLANES:tile, :]
    xc = _silu(conv)
    qk = jnp.dot(xc.astype(BF16), wqk_ref[...], preferred_element_type=F32)
    v = jnp.dot(xm.astype(BF16), wv_ref[...], preferred_element_type=F32)
    q = qk[:, :ML_WP]
    k = qk[:, ML_WP:]
    qkv = jnp.concatenate([q, k, v], axis=1).astype(BF16)
    gates = jnp.dot(qkv, wg_ref[...], preferred_element_type=F32) + bg_ref[...]
    ks = k * (ML_DH ** -0.5)

    row = lax.broadcasted_iota(jnp.int32, (CHUNK, CHUNK), 0)
    col = lax.broadcasted_iota(jnp.int32, (CHUNK, CHUNK), 1)
    causal = col <= row
    lane_h = lax.broadcasted_iota(jnp.int32, (CHUNK, ML_DHP), 1)
    real = lane_h < ML_DH

    for c in range(tile // CHUNK):
        r0 = c * CHUNK
        gc = gates[r0:r0 + CHUNK, :]
        is_f = (col >= ML_HEADS) & (col < 2 * ML_HEADS)
        _, bc = _prefix_levels(jnp.where(is_f, _log_sigmoid(gc), 0.0), row)
        g2 = jnp.where(col < ML_HEADS, gc, bc)
        g2t = g2.T
        outs = []
        for h in range(ML_HEADS):
            c0 = h * ML_DHP
            qh = q[r0:r0 + CHUNK, c0:c0 + ML_DHP]
            kh = ks[r0:r0 + CHUNK, c0:c0 + ML_DHP]
            vh = v[r0:r0 + CHUNK, c0:c0 + ML_DHP].astype(BF16)
            qh_bf = qh.astype(BF16)
            i_col = g2[:, h:h + 1]
            b_col = g2[:, ML_HEADS + h:ML_HEADS + h + 1]
            i_row = g2t[h:h + 1, :]
            b_row = g2t[ML_HEADS + h:ML_HEADS + h + 1, :]
            m_old = m_ref[h:h + 1, 0:1]
            cm = cm_ref[h]
            nvec = n_ref[h:h + 1, :]

            dmat = jnp.where(causal, b_col - b_row + i_row, -jnp.inf)
            inter = b_col + m_old
            m_t = jnp.maximum(inter, jnp.max(dmat, axis=-1, keepdims=True))
            w_intra = jnp.exp(dmat - m_t)
            w_inter = jnp.exp(inter - m_t)
            sm = lax.dot_general(qh_bf, kh.astype(BF16), NT, preferred_element_type=F32) * w_intra
            num = (jnp.dot(sm.astype(BF16), vh, preferred_element_type=F32)
                   + w_inter * jnp.dot(qh_bf, cm.astype(BF16), preferred_element_type=F32))
            den = (jnp.sum(sm, axis=-1, keepdims=True)
                   + w_inter * jnp.sum(qh * nvec, axis=-1, keepdims=True))
            hh = num / jnp.maximum(jnp.abs(den), jnp.exp(-m_t))

            blast = b_col[CHUNK - 1:CHUNK, :]
            g_col = blast - b_col + i_col
            m_new = jnp.maximum(blast + m_old, jnp.max(g_col, axis=0, keepdims=True))
            w_old = jnp.exp(blast + m_old - m_new)
            kw = kh * jnp.exp(g_col - m_new)
            cm_ref[h] = w_old * cm + lax.dot_general(kw.astype(BF16), vh, TN, preferred_element_type=F32)
            n_ref[h:h + 1, :] = w_old * nvec + jnp.sum(kw, axis=0, keepdims=True)
            m_ref[h:h + 1, :] = jnp.broadcast_to(m_new, (1, m_ref.shape[1]))

            mu = jnp.sum(hh, axis=-1, keepdims=True) * (1.0 / ML_DH)
            dlt = jnp.where(real, hh - mu, 0.0)
            var = jnp.sum(dlt * dlt, axis=-1, keepdims=True) * (1.0 / ML_DH)
            outs.append(dlt * lax.rsqrt(var + EPS))
        hn = jnp.concatenate(outs, axis=1)
        y = hn * nw_ref[...] + sk_ref[...] * xc[r0:r0 + CHUNK, :]
        o_ref[r0:r0 + CHUNK, :] = (y * _silu(mz_ref[r0:r0 + CHUNK, :])).astype(o_ref.dtype)


def _mlstm(proj, cw, cb, wqk, wv, wg, bg, nw, sk):
    b, t, _ = proj.shape
    const = lambda bi, ti: (0, 0)
    return pl.pallas_call(
        _mlstm_kernel,
        grid=(b, t // ML_TILE),
        in_specs=[
            pl.BlockSpec((None, ML_TILE, ML_WP), lambda bi, ti: (bi, ti, MX_BLK)),
            pl.BlockSpec((None, ML_TILE, ML_WP), lambda bi, ti: (bi, ti, MZ_BLK)),
            pl.BlockSpec(cw.shape, const),
            pl.BlockSpec(cb.shape, const),
            pl.BlockSpec(wqk.shape, const),
            pl.BlockSpec(wv.shape, const),
            pl.BlockSpec(wg.shape, const),
            pl.BlockSpec(bg.shape, const),
            pl.BlockSpec(nw.shape, const),
            pl.BlockSpec(sk.shape, const),
        ],
        out_specs=pl.BlockSpec((None, ML_TILE, ML_WP), lambda bi, ti: (bi, ti, 0)),
        out_shape=jax.ShapeDtypeStruct((b, t, ML_WP), BF16),
        scratch_shapes=[
            pltpu.VMEM((ML_TILE + SUBLANES, ML_WP), F32),
            pltpu.VMEM((ML_HEADS, ML_DHP, ML_DHP), F32),
            pltpu.VMEM((SUBLANES, ML_DHP), F32),
            pltpu.VMEM((SUBLANES, 128), F32),
        ],
        compiler_params=pltpu.CompilerParams(
            dimension_semantics=("parallel", "arbitrary"), vmem_limit_bytes=VMEM_LIMIT),
        name="mlstm",
    )(proj, proj, cw, cb, wqk, wv, wg, bg, nw, sk)


def _xattn_kernel(q_ref, z_ref, kv_ref, nw_ref, o_ref):
    q = q_ref[...]
    z = z_ref[...]
    outs = []
    for h in range(XA_HEADS):
        c0 = h * XA_DH
        qh = q[:, c0:c0 + XA_DH].astype(BF16)
        kh = kv_ref[:, c0:c0 + XA_DH]
        vh = kv_ref[:, XA_WIDTH + c0:XA_WIDTH + c0 + XA_DH]
        s = lax.dot_general(qh, kh, NT, preferred_element_type=F32) * (XA_DH ** -0.5)
        p = jnp.exp(s - jnp.max(s, axis=-1, keepdims=True))
        p = p / jnp.sum(p, axis=-1, keepdims=True)
        o = jnp.dot(p.astype(BF16), vh, preferred_element_type=F32)
        ms = jnp.mean(o * o, axis=-1, keepdims=True)
        outs.append(o * lax.rsqrt(ms + EPS) * nw_ref[...])
    o = jnp.concatenate(outs, axis=1)
    o_ref[...] = (o * _silu(z)).astype(o_ref.dtype)


def _xattn(proj, mkv, nw):
    b, t, _ = proj.shape
    return pl.pallas_call(
        _xattn_kernel,
        grid=(b, t // XA_TILE),
        in_specs=[
            pl.BlockSpec((None, XA_TILE, XA_WIDTH), lambda bi, ti: (bi, ti, AQ_BLK)),
            pl.BlockSpec((None, XA_TILE, XA_WIDTH), lambda bi, ti: (bi, ti, AZ_BLK)),
            pl.BlockSpec((None, MEM_LEN, 2 * XA_WIDTH), lambda bi, ti: (bi, 0, 0)),
            pl.BlockSpec((1, XA_DH), lambda bi, ti: (0, 0)),
        ],
        out_specs=pl.BlockSpec((None, XA_TILE, XA_WIDTH), lambda bi, ti: (bi, ti, 0)),
        out_shape=jax.ShapeDtypeStruct((b, t, XA_WIDTH), BF16),
        compiler_params=pltpu.CompilerParams(
            dimension_semantics=("parallel", "parallel"), vmem_limit_bytes=VMEM_LIMIT),
        name="xattn",
    )(proj, proj, mkv, nw)


def _pad_heads(a, axis):
    a = jnp.moveaxis(a, axis, -1)
    lead = a.shape[:-1]
    a = a.reshape(*lead, ML_HEADS, ML_DH)
    a = jnp.pad(a, [(0, 0)] * len(lead) + [(0, 0), (0, ML_DHP - ML_DH)])
    return jnp.moveaxis(a.reshape(*lead, ML_WP), -1, axis)


def _block_diag(w):
    eye = jnp.eye(ML_NBLK, dtype=w.dtype)
    dense = (eye[:, None, :, None] * w[:, :, None, :]).reshape(ML_WIDTH, ML_WIDTH)
    return _pad_heads(_pad_heads(dense, 0), 1)


def kernel(x, mem, norm_w, w_in, hgrn_lb_logits, hgrn_norm_w, mlstm_conv_w, mlstm_conv_b,
           mlstm_wq, mlstm_wk, mlstm_wv, mlstm_w_gate, mlstm_b_gate, mlstm_norm_w, mlstm_skip,
           mem_norm_w, w_kv_mem, xattn_norm_w, w_out, final_norm_w):
    bsz, seq, d = x.shape
    n = bsz * seq
    lb_all = jnp.cumsum(jax.nn.softmax(hgrn_lb_logits.astype(F32), axis=0), axis=0)
    lb_all = jnp.maximum(lb_all - lb_all[0:1], 0.0)

    x2 = x.reshape(n, d)
    mem2 = mem.reshape(bsz * MEM_LEN, d)
    hg_end = 4 * HG_WIDTH
    ml_end = hg_end + 2 * ML_WIDTH
    for l in range(DEPTH):
        wl = w_in[l]
        w_pad = jnp.concatenate([
            wl[:, :hg_end],
            _pad_heads(wl[:, hg_end:hg_end + ML_WIDTH], 1),
            _pad_heads(wl[:, hg_end + ML_WIDTH:ml_end], 1),
            wl[:, ml_end:],
        ], axis=1).astype(BF16)
        proj = _inproj(x2, norm_w[l][None, :], w_pad).reshape(bsz, seq, P_PAD)

        o_hg = _hgrn(proj, lb_all[l][None, :], hgrn_norm_w[l][None, :])

        wqk = jnp.concatenate([_block_diag(mlstm_wq[l]), _block_diag(mlstm_wk[l])], axis=1).astype(BF16)
        wv = _block_diag(mlstm_wv[l]).astype(BF16)
        wg = mlstm_w_gate[l].reshape(3, ML_WIDTH, 2 * ML_HEADS)
        wg = _pad_heads(wg, 1).reshape(3 * ML_WP, 2 * ML_HEADS)
        wg = jnp.pad(wg, ((0, 0), (0, 128 - 2 * ML_HEADS))).astype(BF16)
        bg = jnp.pad(mlstm_b_gate[l], (0, 128 - 2 * ML_HEADS))[None, :]
        o_ml = _mlstm(proj, _pad_heads(mlstm_conv_w[l], 1), _pad_heads(mlstm_conv_b[l], 0)[None, :],
                      wqk, wv, wg, bg, _pad_heads(mlstm_norm_w[l], 0)[None, :],
                      _pad_heads(mlstm_skip[l], 0)[None, :])

        mkv = _memkv(mem2, mem_norm_w[l][None, :], w_kv_mem[l].astype(BF16))
        o_xa = _xattn(proj, mkv.reshape(bsz, MEM_LEN, 2 * XA_WIDTH), xattn_norm_w[l][None, :])

        wo = w_out[l]
        w1 = wo[:HG_WIDTH].astype(BF16)
        w2 = _pad_heads(wo[HG_WIDTH:HG_WIDTH + ML_WIDTH], 0).astype(BF16)
        w3 = wo[HG_WIDTH + ML_WIDTH:].astype(BF16)
        x2 = _outproj(o_hg.reshape(n, HG_WIDTH), o_ml.reshape(n, ML_WP), o_xa.reshape(n, XA_WIDTH),
                      x2, w1, w2, w3, final_norm_w[None, :], final=(l == DEPTH - 1))
    return x2.reshape(bsz, seq, d)
```

```python
import functools

import jax
import jax.numpy as jnp
from jax import lax
from jax.experimental import pallas as pl
from jax.experimental.pallas import tpu as pltpu

F32 = jnp.float32
BF16 = jnp.bfloat16

D_MODEL = 1024
DEPTH = 2
MEM_LEN = 256
HG_DK = 128
HG_WIDTH = 768
HG_HEADS = 6
ML_WIDTH = 768
ML_HEADS = 4
ML_DH = 192
ML_DHP = 256
ML_WP = ML_HEADS * ML_DHP
ML_BLOCK = 4
ML_CONV = 4
N_GATES = 2 * ML_HEADS
XA_WIDTH = 512
XA_HEADS = 4
XA_DH = 128
EPS = 1e-6
M_INIT = -1e30

LANES = 128
SUBLANES = 8
CHUNK = 128

P_PAD = 4 * HG_WIDTH + 2 * ML_WP + 2 * XA_WIDTH
HG_COLBLK = HG_WIDTH // HG_DK
MX_BLK = (4 * HG_WIDTH) // ML_WP
MZ_BLK = MX_BLK + 1
AQ_BLK = (4 * HG_WIDTH + 2 * ML_WP) // XA_WIDTH
AZ_BLK = AQ_BLK + 1

ROW_TILE = 256
HG_TILE = 512
ML_TILE = 256
XA_TILE = 512
VMEM_LIMIT = 56 * 1024 * 1024

NT = (((1,), (1,)), ((), ()))
TN = (((0,), (0,)), ((), ()))


def _silu(x):
    return x / (1.0 + jnp.exp(-x))


def _log_sigmoid(x):
    return jnp.minimum(x, 0.0) - jnp.log(1.0 + jnp.exp(-jnp.abs(x)))


def _bcast_block_row(x, blk, r):
    n = x.shape[0] // blk
    x3 = x.reshape(n, blk, x.shape[1])
    return jnp.broadcast_to(x3[:, r:r + 1, :], x3.shape).reshape(x.shape)


def _left_total(p, m, row):
    if m == 1:
        return jnp.where((row & 1) != 0, pltpu.roll(p, 1, axis=0), p)
    if 2 * m < SUBLANES:
        lo = _bcast_block_row(p, SUBLANES, m - 1)
        hi = _bcast_block_row(p, SUBLANES, 2 * m + m - 1)
        return jnp.where((row & (SUBLANES - 1)) < 2 * m, lo, hi)
    return _bcast_block_row(p, 2 * m, m - 1)


def _decay_levels(g, row):
    p = g
    out = []
    m = 1
    while m < g.shape[0]:
        right = (row & m) != 0
        lt = _left_total(p, m, row)
        out.append((m, jnp.where(right, p, lt - p)))
        p = p + jnp.where(right, lt, 0.0)
        m *= 2
    return out, p


def _inproj_kernel(x_ref, nw_ref, w_ref, o_ref):
    x = x_ref[...]
    ms = jnp.mean(x * x, axis=-1, keepdims=True)
    h = x * lax.rsqrt(ms + EPS) * nw_ref[...]
    o_ref[...] = jnp.dot(h.astype(BF16), w_ref[...], preferred_element_type=F32).astype(o_ref.dtype)


def _inproj(x2, nw, w, out_dtype, name):
    n, d = x2.shape
    p = w.shape[1]
    return pl.pallas_call(
        _inproj_kernel,
        grid=(n // ROW_TILE,),
        in_specs=[
            pl.BlockSpec((ROW_TILE, d), lambda i: (i, 0)),
            pl.BlockSpec((1, d), lambda i: (0, 0)),
            pl.BlockSpec((d, p), lambda i: (0, 0)),
        ],
        out_specs=pl.BlockSpec((ROW_TILE, p), lambda i: (i, 0)),
        out_shape=jax.ShapeDtypeStruct((n, p), out_dtype),
        compiler_params=pltpu.CompilerParams(
            dimension_semantics=("parallel",), vmem_limit_bytes=VMEM_LIMIT),
        name=name,
    )(x2, nw, w)


def _outproj_kernel(hg_ref, ml_ref, xa_ref, x_ref, w1_ref, w2_ref, w3_ref, fw_ref, o_ref, *, final):
    y = jnp.dot(hg_ref[...], w1_ref[...], preferred_element_type=F32)
    y = y + jnp.dot(ml_ref[...], w2_ref[...], preferred_element_type=F32)
    y = y + jnp.dot(xa_ref[...], w3_ref[...], preferred_element_type=F32)
    x = x_ref[...] + y
    if final:
        ms = jnp.mean(x * x, axis=-1, keepdims=True)
        x = x * lax.rsqrt(ms + EPS) * fw_ref[...]
    o_ref[...] = x


def _outproj(o_hg, o_ml, o_xa, x2, w1, w2, w3, fw, final):
    n, d = x2.shape
    row = lambda i: (i, 0)
    const = lambda i: (0, 0)
    return pl.pallas_call(
        functools.partial(_outproj_kernel, final=final),
        grid=(n // ROW_TILE,),
        in_specs=[
            pl.BlockSpec((ROW_TILE, o_hg.shape[1]), row),
            pl.BlockSpec((ROW_TILE, o_ml.shape[1]), row),
            pl.BlockSpec((ROW_TILE, o_xa.shape[1]), row),
            pl.BlockSpec((ROW_TILE, d), row),
            pl.BlockSpec(w1.shape, const),
            pl.BlockSpec(w2.shape, const),
            pl.BlockSpec(w3.shape, const),
            pl.BlockSpec((1, d), const),
        ],
        out_specs=pl.BlockSpec((ROW_TILE, d), row),
        out_shape=jax.ShapeDtypeStruct((n, d), F32),
        compiler_params=pltpu.CompilerParams(
            dimension_semantics=("parallel",), vmem_limit_bytes=VMEM_LIMIT),
        name="outproj",
    )(o_hg, o_ml, o_xa, x2, w1, w2, w3, fw)


def _hgrn_kernel(q_ref, f_ref, i_ref, g_ref, lb_ref, nw_ref, o_ref, st_ref, *, has_lb):
    @pl.when(pl.program_id(2) == 0)
    def _():
        st_ref[...] = jnp.zeros_like(st_ref)

    if has_lb:
        lb = lb_ref[...]
        log_lb = jnp.log(lb)
        log_1m = jnp.log(1.0 - lb)
    row = lax.broadcasted_iota(jnp.int32, (CHUNK, HG_DK), 0)
    rowc = lax.broadcasted_iota(jnp.int32, (CHUNK, CHUNK), 0)
    col = lax.broadcasted_iota(jnp.int32, (CHUNK, CHUNK), 1)
    pair_bits = rowc ^ col

    for c in range(q_ref.shape[0] // CHUNK):
        rows = pl.ds(c * CHUNK, CHUNK)
        q = _silu(q_ref[rows, :]) * (HG_DK ** -0.5)
        f_pre = f_ref[rows, :]
        v_bf = i_ref[rows, :].astype(BF16)
        e = jnp.exp(-jnp.abs(f_pre))
        k = jnp.where(f_pre >= 0.0, e, 1.0) / (1.0 + e)
        g = jnp.minimum(f_pre, 0.0) - jnp.log(1.0 + e)
        if has_lb:
            k = (1.0 - lb) * k
            c1 = log_1m + g
            g = jnp.maximum(log_lb, c1) + jnp.log(1.0 + jnp.exp(-jnp.abs(log_lb - c1)))

        levels, b = _decay_levels(g, row)
        q_bf = q.astype(BF16)
        k_bf = k.astype(BF16)

        (_, e1), rest = levels[0], levels[1:]
        lhs = jnp.concatenate([q_bf, (q * jnp.exp(e1)).astype(BF16)], axis=0)
        a01 = lax.dot_general(lhs, k_bf, NT, preferred_element_type=F32)
        a = None
        for m, em in reversed(rest):
            ex = jnp.exp(em)
            al = lax.dot_general((q * ex).astype(BF16), (k * ex).astype(BF16), NT,
                                 preferred_element_type=F32)
            a = al if a is None else jnp.where(pair_bits < 2 * m, al, a)
        a = jnp.where(pair_bits < 2, a01[CHUNK:, :], a)
        a = jnp.where(pair_bits == 0, a01[:CHUNK, :], a)
        a = jnp.where(rowc >= col, a, 0.0)
        o = jnp.dot(a.astype(BF16), v_bf, preferred_element_type=F32)

        st = st_ref[...]
        qb = (q * jnp.exp(b)).astype(BF16)
        o = o + lax.dot_general(qb, st.astype(BF16), NT, preferred_element_type=F32)
        blast = b[CHUNK - 1:CHUNK, :]
        kd = (k * jnp.exp(blast - b)).astype(BF16)
        st_ref[...] = st * jnp.exp(blast) + lax.dot_general(v_bf, kd, TN, preferred_element_type=F32)

        ms = jnp.mean(o * o, axis=-1, keepdims=True)
        o = o * lax.rsqrt(ms + EPS) * nw_ref[...]
        o_ref[rows, :] = (o * _silu(g_ref[rows, :])).astype(o_ref.dtype)


def _hgrn(proj, lb, nw, has_lb):
    b, t, _ = proj.shape
    sec = lambda s: pl.BlockSpec((None, HG_TILE, HG_DK), lambda bi, h, ti, s=s: (bi, ti, s * HG_COLBLK + h))
    return pl.pallas_call(
        functools.partial(_hgrn_kernel, has_lb=has_lb),
        grid=(b, HG_HEADS, t // HG_TILE),
        in_specs=[
            sec(0), sec(1), sec(2), sec(3),
            pl.BlockSpec((1, HG_DK), lambda bi, h, ti: (0, h)),
            pl.BlockSpec((1, HG_DK), lambda bi, h, ti: (0, 0)),
        ],
        out_specs=pl.BlockSpec((None, HG_TILE, HG_DK), lambda bi, h, ti: (bi, ti, h)),
        out_shape=jax.ShapeDtypeStruct((b, t, HG_WIDTH), BF16),
        scratch_shapes=[pltpu.VMEM((HG_DK, HG_DK), F32)],
        compiler_params=pltpu.CompilerParams(
            dimension_semantics=("parallel", "parallel", "arbitrary"), vmem_limit_bytes=VMEM_LIMIT),
        name="hgrn2",
    )(proj, proj, proj, proj, lb, nw)


def _mlstm_kernel(mx_ref, mz_ref, cw_ref, cb_ref, wqk_ref, wv_ref, wg_ref, bg_ref, nw_ref, sk_ref,
                  o_ref, xbuf, cm_ref, n_ref, m_ref):
    tile = mx_ref.shape[0]

    @pl.when(pl.program_id(1) == 0)
    def _():
        xbuf[0:SUBLANES, :] = jnp.zeros((SUBLANES, ML_WP), F32)
        cm_ref[...] = jnp.zeros_like(cm_ref)
        n_ref[...] = jnp.zeros_like(n_ref)
        m_ref[...] = jnp.full(m_ref.shape, M_INIT, F32)

    xm = mx_ref[...]
    xbuf[SUBLANES:SUBLANES + tile, :] = xm
    conv = cb_ref[...]
    for j in range(ML_CONV):
        conv = conv + cw_ref[j:j + 1, :] * xbuf[pl.ds(SUBLANES - (ML_CONV - 1) + j, tile), :]
    xbuf[0:SUBLANES, :] = xm[tile - SUBLANES:tile, :]
    xc = _silu(conv)

    q_bf, k_bf, v_bf, k_f32 = [], [], [], []
    gates = bg_ref[...]
    for h in range(ML_HEADS):
        c0 = h * ML_DHP
        qk = jnp.dot(xc[:, c0:c0 + ML_DHP].astype(BF16), wqk_ref[h], preferred_element_type=F32)
        vv = jnp.dot(xm[:, c0:c0 + ML_DHP].astype(BF16), wv_ref[h], preferred_element_type=F32)
        q_bf.append(qk[:, :ML_DHP].astype(BF16))
        k_f32.append(qk[:, ML_DHP:])
        k_bf.append(k_f32[h].astype(BF16))
        v_bf.append(vv.astype(BF16))
        for part, op in enumerate((q_bf[h], k_bf[h], v_bf[h])):
            r0 = part * ML_WP + c0
            gates = gates + jnp.dot(op, wg_ref[r0:r0 + ML_DHP, :], preferred_element_type=F32)

    row = lax.broadcasted_iota(jnp.int32, (CHUNK, CHUNK), 0)
    col = lax.broadcasted_iota(jnp.int32, (CHUNK, CHUNK), 1)
    causal = col <= row
    is_f = (col >= ML_HEADS) & (col < N_GATES)
    real = lax.broadcasted_iota(jnp.int32, (CHUNK, ML_DHP), 1) < ML_DH
    kscale = ML_DH ** -0.5

    for c in range(tile // CHUNK):
        r0 = c * CHUNK
        gc = gates[r0:r0 + CHUNK, :]
        _, bc = _decay_levels(jnp.where(is_f, _log_sigmoid(gc), 0.0), row)
        g2 = jnp.where(col < ML_HEADS, gc, bc)
        g2t = g2.T
        outs = []
        for h in range(ML_HEADS):
            qh = q_bf[h][r0:r0 + CHUNK, :]
            kh = k_bf[h][r0:r0 + CHUNK, :]
            vh = v_bf[h][r0:r0 + CHUNK, :]
            i_col = g2[:, h:h + 1]
            b_col = g2[:, ML_HEADS + h:ML_HEADS + h + 1]
            i_row = g2t[h:h + 1, :]
            b_row = g2t[ML_HEADS + h:ML_HEADS + h + 1, :]
            m_old = m_ref[h:h + 1, 0:1]
            cm = cm_ref[h]
            nvec = n_ref[h:h + 1, :]

            dmat = jnp.where(causal, b_col - b_row + i_row, -jnp.inf)
            inter = b_col + m_old
            m_t = jnp.maximum(inter, jnp.max(dmat, axis=-1, keepdims=True))
            w_intra = jnp.exp(dmat - m_t)
            w_inter = jnp.exp(inter - m_t)
            sm = lax.dot_general(qh, kh, NT, preferred_element_type=F32) * kscale * w_intra
            num = (jnp.dot(sm.astype(BF16), vh, preferred_element_type=F32)
                   + w_inter * jnp.dot(qh, cm.astype(BF16), preferred_element_type=F32))
            den = (jnp.sum(sm, axis=-1, keepdims=True)
                   + w_inter * jnp.sum(qh.astype(F32) * nvec, axis=-1, keepdims=True))
            hh = num / jnp.maximum(jnp.abs(den), jnp.exp(-m_t))

            blast = b_col[CHUNK - 1:CHUNK, :]
            g_col = blast - b_col + i_col
            m_new = jnp.maximum(blast + m_old, jnp.max(g_col, axis=0, keepdims=True))
            w_old = jnp.exp(blast + m_old - m_new)
            kw = k_f32[h][r0:r0 + CHUNK, :] * (jnp.exp(g_col - m_new) * kscale)
            cm_ref[h] = w_old * cm + lax.dot_general(kw.astype(BF16), vh, TN, preferred_element_type=F32)
            n_ref[h:h + 1, :] = w_old * nvec + jnp.sum(kw, axis=0, keepdims=True)
            m_ref[h:h + 1, :] = jnp.broadcast_to(m_new, (1, m_ref.shape[1]))

            mu = jnp.sum(hh, axis=-1, keepdims=True) * (1.0 / ML_DH)
            dlt = jnp.where(real, hh - mu, 0.0)
            var = jnp.sum(dlt * dlt, axis=-1, keepdims=True) * (1.0 / ML_DH)
            outs.append(dlt * lax.rsqrt(var + EPS))
        hn = jnp.concatenate(outs, axis=1)
        y = hn * nw_ref[...] + sk_ref[...] * xc[r0:r0 + CHUNK, :]
        o_ref[r0:r0 + CHUNK, :] = (y * _silu(mz_ref[r0:r0 + CHUNK, :])).astype(o_ref.dtype)


def _mlstm(proj, cw, cb, wqk, wv, wg, bg, nw, sk):
    b, t, _ = proj.shape
    const2 = lambda bi, ti: (0, 0)
    const3 = lambda bi, ti: (0, 0, 0)
    return pl.pallas_call(
        _mlstm_kernel,
        grid=(b, t // ML_TILE),
        in_specs=[
            pl.BlockSpec((None, ML_TILE, ML_WP), lambda bi, ti: (bi, ti, MX_BLK)),
            pl.BlockSpec((None, ML_TILE, ML_WP), lambda bi, ti: (bi, ti, MZ_BLK)),
            pl.BlockSpec(cw.shape, const2),
            pl.BlockSpec(cb.shape, const2),
            pl.BlockSpec(wqk.shape, const3),
            pl.BlockSpec(wv.shape, const3),
            pl.BlockSpec(wg.shape, const2),
            pl.BlockSpec(bg.shape, const2),
            pl.BlockSpec(nw.shape, const2),
            pl.BlockSpec(sk.shape, const2),
        ],
        out_specs=pl.BlockSpec((None, ML_TILE, ML_WP), lambda bi, ti: (bi, ti, 0)),
        out_shape=jax.ShapeDtypeStruct((b, t, ML_WP), BF16),
        scratch_shapes=[
            pltpu.VMEM((ML_TILE + SUBLANES, ML_WP), F32),
            pltpu.VMEM((ML_HEADS, ML_DHP, ML_DHP), F32),
            pltpu.VMEM((SUBLANES, ML_DHP), F32),
            pltpu.VMEM((SUBLANES, LANES), F32),
        ],
        compiler_params=pltpu.CompilerParams(
            dimension_semantics=("parallel", "arbitrary"), vmem_limit_bytes=VMEM_LIMIT),
        name="mlstm",
    )(proj, proj, cw, cb, wqk, wv, wg, bg, nw, sk)


def _xattn_kernel(q_ref, z_ref, kv_ref, nw_ref, o_ref):
    q = q_ref[...]
    z = z_ref[...]
    outs = []
    for h in range(XA_HEADS):
        c0 = h * XA_DH
        qh = q[:, c0:c0 + XA_DH].astype(BF16)
        kh = kv_ref[:, c0:c0 + XA_DH]
        vh = kv_ref[:, XA_WIDTH + c0:XA_WIDTH + c0 + XA_DH]
        s = lax.dot_general(qh, kh, NT, preferred_element_type=F32) * (XA_DH ** -0.5)
        p = jnp.exp(s - jnp.max(s, axis=-1, keepdims=True))
        p = p / jnp.sum(p, axis=-1, keepdims=True)
        o = jnp.dot(p.astype(BF16), vh, preferred_element_type=F32)
        ms = jnp.mean(o * o, axis=-1, keepdims=True)
        outs.append(o * lax.rsqrt(ms + EPS) * nw_ref[...])
    o = jnp.concatenate(outs, axis=1)
    o_ref[...] = (o * _silu(z)).astype(o_ref.dtype)


def _xattn(proj, mkv, nw):
    b, t, _ = proj.shape
    return pl.pallas_call(
        _xattn_kernel,
        grid=(b, t // XA_TILE),
        in_specs=[
            pl.BlockSpec((None, XA_TILE, XA_WIDTH), lambda bi, ti: (bi, ti, AQ_BLK)),
            pl.BlockSpec((None, XA_TILE, XA_WIDTH), lambda bi, ti: (bi, ti, AZ_BLK)),
            pl.BlockSpec((None, MEM_LEN, 2 * XA_WIDTH), lambda bi, ti: (bi, 0, 0)),
            pl.BlockSpec((1, XA_DH), lambda bi, ti: (0, 0)),
        ],
        out_specs=pl.BlockSpec((None, XA_TILE, XA_WIDTH), lambda bi, ti: (bi, ti, 0)),
        out_shape=jax.ShapeDtypeStruct((b, t, XA_WIDTH), BF16),
        compiler_params=pltpu.CompilerParams(
            dimension_semantics=("parallel", "parallel"), vmem_limit_bytes=VMEM_LIMIT),
        name="xattn",
    )(proj, proj, mkv, nw)


def _pad_heads(a, axis):
    a = jnp.moveaxis(a, axis, -1)
    lead = a.shape[:-1]
    a = a.reshape(*lead, ML_HEADS, ML_DH)
    a = jnp.pad(a, [(0, 0)] * len(lead) + [(0, 0), (0, ML_DHP - ML_DH)])
    return jnp.moveaxis(a.reshape(*lead, ML_WP), -1, axis)


def _head_block_diag(w):
    per_head = ML_DH // ML_BLOCK
    w = w.reshape(ML_HEADS, per_head, ML_BLOCK, ML_BLOCK)
    eye = jnp.eye(per_head, dtype=w.dtype)
    dense = (eye[None, :, None, :, None] * w[:, :, :, None, :]).reshape(ML_HEADS, ML_DH, ML_DH)
    pad = ML_DHP - ML_DH
    return jnp.pad(dense, ((0, 0), (0, pad), (0, pad)))


def kernel(x, mem, norm_w, w_in, hgrn_lb_logits, hgrn_norm_w, mlstm_conv_w, mlstm_conv_b,
           mlstm_wq, mlstm_wk, mlstm_wv, mlstm_w_gate, mlstm_b_gate, mlstm_norm_w, mlstm_skip,
           mem_norm_w, w_kv_mem, xattn_norm_w, w_out, final_norm_w):
    bsz, seq, d = x.shape
    n = bsz * seq
    lb_all = jnp.cumsum(jax.nn.softmax(hgrn_lb_logits.astype(F32), axis=0), axis=0)
    lb_all = jnp.maximum(lb_all - lb_all[0:1], 0.0)

    x2 = x.reshape(n, d)
    mem2 = mem.reshape(bsz * MEM_LEN, d)
    hg_end = 4 * HG_WIDTH
    ml_end = hg_end + 2 * ML_WIDTH
    for l in range(DEPTH):
        wl = w_in[l]
        w_pad = jnp.concatenate([
            wl[:, :hg_end],
            _pad_heads(wl[:, hg_end:hg_end + ML_WIDTH], 1),
            _pad_heads(wl[:, hg_end + ML_WIDTH:ml_end], 1),
            wl[:, ml_end:],
        ], axis=1).astype(BF16)
        proj = _inproj(x2, norm_w[l][None, :], w_pad, F32, "inproj").reshape(bsz, seq, P_PAD)

        o_hg = _hgrn(proj, lb_all[l][None, :], hgrn_norm_w[l][None, :], has_lb=(l > 0))

        wqk = jnp.concatenate([_head_block_diag(mlstm_wq[l]), _head_block_diag(mlstm_wk[l])],
                              axis=2).astype(BF16)
        wv = _head_block_diag(mlstm_wv[l]).astype(BF16)
        wg = mlstm_w_gate[l].reshape(3, ML_WIDTH, N_GATES)
        wg = _pad_heads(wg, 1).reshape(3 * ML_WP, N_GATES)
        wg = jnp.pad(wg, ((0, 0), (0, LANES - N_GATES))).astype(BF16)
        bg = jnp.pad(mlstm_b_gate[l], (0, LANES - N_GATES))[None, :]
        o_ml = _mlstm(proj, _pad_heads(mlstm_conv_w[l], 1), _pad_heads(mlstm_conv_b[l], 0)[None, :],
                      wqk, wv, wg, bg, _pad_heads(mlstm_norm_w[l], 0)[None, :],
                      _pad_heads(mlstm_skip[l], 0)[None, :])

        mkv = _inproj(mem2, mem_norm_w[l][None, :], w_kv_mem[l].astype(BF16), BF16, "memkv")
        o_xa = _xattn(proj, mkv.reshape(bsz, MEM_LEN, 2 * XA_WIDTH), xattn_norm_w[l][None, :])

        wo = w_out[l]
        w1 = wo[:HG_WIDTH].astype(BF16)
        w2 = _pad_heads(wo[HG_WIDTH:HG_WIDTH + ML_WIDTH], 0).astype(BF16)
        w3 = wo[HG_WIDTH + ML_WIDTH:].astype(BF16)
        x2 = _outproj(o_hg.reshape(n, HG_WIDTH), o_ml.reshape(n, ML_WP), o_xa.reshape(n, XA_WIDTH),
                      x2, w1, w2, w3, final_norm_w[None, :], final=(l == DEPTH - 1))
    return x2.reshape(bsz, seq, d)
```
